```python
import jax, jax.numpy as jnp
from jax import lax
import numpy as np

D_MODEL = 1024
BATCH = 8
SEQ = 4096
DEPTH = 2

EXPAND = 2
D_INNER = EXPAND * D_MODEL
N_GROUPS = 8
GROUP_DIM = D_INNER // N_GROUPS
CHUNK = 128
N_MIXERS = 2
EPS = 1e-6

kernel_name = "hybrid_fourier_sgu_encoder"


def rms_norm(x, g):
    xf = x.astype(jnp.float32)
    y = xf * lax.rsqrt(jnp.mean(xf * xf, axis=-1, keepdims=True) + EPS)
    return (y * g.astype(jnp.float32)).astype(x.dtype)


def layer_norm(x, g, b):
    xf = x.astype(jnp.float32)
    mu = jnp.mean(xf, axis=-1, keepdims=True)
    var = jnp.mean(jnp.square(xf - mu), axis=-1, keepdims=True)
    y = (xf - mu) * lax.rsqrt(var + EPS)
    return (y * g.astype(jnp.float32) + b.astype(jnp.float32)).astype(x.dtype)


def fourier_mixer(h, w_in, w_out):
    b, s, _ = h.shape
    proj = h @ w_in
    xin, z = proj[..., :D_INNER], proj[..., D_INNER:]
    xg = xin.reshape(b, s, N_GROUPS, GROUP_DIM).astype(jnp.float32)
    y = jnp.fft.fft2(xg, axes=(1, 3), norm="ortho").real
    y = y.astype(h.dtype).reshape(b, s, D_INNER)
    return (y * jax.nn.silu(z)) @ w_out


def sgu_mixer(h, w_in, v_ln_g, v_ln_b, w_s, b_s, w_out):
    b, s, _ = h.shape
    proj = h @ w_in
    uv = jax.nn.gelu(proj[..., :2 * D_INNER])
    z = proj[..., 2 * D_INNER:]
    u, v = uv[..., :D_INNER], uv[..., D_INNER:]
    v = layer_norm(v, v_ln_g, v_ln_b)
    v = v.reshape(b, s // CHUNK, CHUNK, N_GROUPS, GROUP_DIM)
    v = jnp.einsum('gpq,bnqgc->bnpgc', w_s, v) + b_s.T[None, None, :, :, None]
    y = u * v.reshape(b, s, D_INNER)
    return (y * jax.nn.silu(z)) @ w_out


def setup_inputs(seed: int = 0) -> dict:
    key = jax.random.key(seed)
    ks = jax.random.split(key, 12)
    f32 = jnp.float32
    d, e = D_MODEL, D_INNER
    return {
        "x": jax.random.normal(ks[0], (BATCH, SEQ, d), f32),
        "l0_norm": 1.0 + 0.02 * jax.random.normal(ks[1], (d,), f32),
        "l0_w_in": jax.random.normal(ks[2], (d, 2 * e), f32) * d ** -0.5,
        "l0_w_out": jax.random.normal(ks[3], (e, d), f32) * e ** -0.5,
        "l1_norm": 1.0 + 0.02 * jax.random.normal(ks[4], (d,), f32),
        "l1_w_in": jax.random.normal(ks[5], (d, 3 * e), f32) * d ** -0.5,
        "l1_v_ln_g": 1.0 + 0.02 * jax.random.normal(ks[6], (e,), f32),
        "l1_v_ln_b": 0.02 * jax.random.normal(ks[7], (e,), f32),
        "l1_w_s": jax.random.normal(ks[8], (N_GROUPS, CHUNK, CHUNK), f32) * CHUNK ** -0.5,
        "l1_b_s": 1.0 + 0.02 * jax.random.normal(ks[9], (N_GROUPS, CHUNK), f32),
        "l1_w_out": jax.random.normal(ks[10], (e, d), f32) * e ** -0.5,
        "final_norm": 1.0 + 0.02 * jax.random.normal(ks[11], (d,), f32),
    }


def reference(x, l0_norm, l0_w_in, l0_w_out, l1_norm, l1_w_in, l1_v_ln_g, l1_v_ln_b,
              l1_w_s, l1_b_s, l1_w_out, final_norm):
    mixers = [fourier_mixer, sgu_mixer]
    norms = [l0_norm, l1_norm]
    params = [
        (l0_w_in, l0_w_out),
        (l1_w_in, l1_v_ln_g, l1_v_ln_b, l1_w_s, l1_b_s, l1_w_out),
    ]
    for i in range(DEPTH):
        h = rms_norm(x, norms[i])
        x = x + mixers[i % N_MIXERS](h, *params[i])
    return rms_norm(x, final_norm)
```

```python
import functools
import math

import numpy as np
import jax
import jax.numpy as jnp
from jax import lax
from jax.experimental import pallas as pl
from jax.experimental.pallas import tpu as pltpu

D_MODEL = 1024
D_INNER = 2048
N_GROUPS = 8
GROUP_DIM = 256
CHUNK = 128
EPS = 1e-6

SEQ = 4096
FFT_N1 = 256
FFT_N2 = 16

TM_PROJ = 512
TM_TRUNK = 512

F32 = jnp.float32
BF16 = jnp.bfloat16

_TN_DIMS = (((0,), (0,)), ((), ()))


def _rms_norm(x, w):
    ms = jnp.mean(x * x, axis=-1, keepdims=True)
    return x * lax.rsqrt(ms + EPS) * w


def _silu(z):
    return z * jax.nn.sigmoid(z)


def _gelu_tanh(x):
    c = math.sqrt(2.0 / math.pi)
    return x * (0.5 * (1.0 + jnp.tanh(c * (x + 0.044715 * (x * x * x)))))


def _in_proj0_body(x_ref, nw_ref, w_ref, xin_ref, sz_ref):
    h = _rms_norm(x_ref[...], nw_ref[...]).astype(BF16)
    half = D_INNER // 2
    for j in range(2):
        p = jnp.dot(h, w_ref[:, j * half:(j + 1) * half], preferred_element_type=F32)
        xin_ref[:, j * half:(j + 1) * half] = p.astype(BF16)
    for j in range(2):
        z = jnp.dot(h, w_ref[:, D_INNER + j * half:D_INNER + (j + 1) * half],
                    preferred_element_type=F32)
        sz_ref[:, j * half:(j + 1) * half] = _silu(z).astype(BF16)


def _in_proj0(x2d, norm_w, w_in):
    t = x2d.shape[0]
    const = lambda i: (0, 0)
    return pl.pallas_call(
        _in_proj0_body,
        grid=(t // TM_PROJ,),
        in_specs=[
            pl.BlockSpec((TM_PROJ, D_MODEL), lambda i: (i, 0)),
            pl.BlockSpec((1, D_MODEL), const),
            pl.BlockSpec((D_MODEL, 2 * D_INNER), const, pipeline_mode=pl.Buffered(1)),
        ],
        out_specs=[
            pl.BlockSpec((TM_PROJ, D_INNER), lambda i: (i, 0)),
            pl.BlockSpec((TM_PROJ, D_INNER), lambda i: (i, 0)),
        ],
        out_shape=[
            jax.ShapeDtypeStruct((t, D_INNER), BF16),
            jax.ShapeDtypeStruct((t, D_INNER), BF16),
        ],
        compiler_params=pltpu.CompilerParams(
            dimension_semantics=("arbitrary",), vmem_limit_bytes=56 * 1024 * 1024),
        name="in_proj0",
    )(x2d, norm_w, w_in)


def _fft_radix2(re, im):
    n = len(re)
    if n == 1:
        return re, im
    er, ei = _fft_radix2(re[0::2], im[0::2])
    orr, oi = _fft_radix2(re[1::2], im[1::2])
    out_r = [None] * n
    out_i = [None] * n
    h = n // 2
    for k in range(h):
        if k == 0:
            tr, ti = orr[k], oi[k]
        elif 4 * k == n:
            tr, ti = oi[k], -orr[k]
        elif 8 * k == n:
            s = math.sqrt(0.5)
            tr, ti = (orr[k] + oi[k]) * s, (oi[k] - orr[k]) * s
        elif 8 * k == 3 * n:
            s = math.sqrt(0.5)
            tr, ti = (oi[k] - orr[k]) * s, -(orr[k] + oi[k]) * s
        else:
            c = math.cos(2.0 * math.pi * k / n)
            s = math.sin(2.0 * math.pi * k / n)
            tr = orr[k] * c + oi[k] * s
            ti = oi[k] * c - orr[k] * s
        out_r[k] = er[k] + tr
        out_i[k] = ei[k] + ti
        out_r[k + h] = er[k] - tr
        out_i[k + h] = ei[k] - ti
    return out_r, out_i


def _fft2_body(*refs):
    x_refs = refs[:FFT_N2]
    g_ref, cs_ref, y_ref, t_ref, u_ref = refs[FFT_N2:]

    for n2 in range(FFT_N2):
        t_ref[n2] = lax.dot_general(x_refs[n2][...], g_ref[n2], _TN_DIMS,
                                    preferred_element_type=F32)

    def rows(i, carry):
        r = pl.multiple_of(i * 8, 8)
        for lh in range(FFT_N1 // 128):
            lo = lh * 128
            re = [t_ref[n2, pl.ds(r, 8), lo:lo + 128] for n2 in range(FFT_N2)]
            im = [t_ref[n2, pl.ds(r, 8), FFT_N1 + lo:FFT_N1 + lo + 128]
                  for n2 in range(FFT_N2)]
            out_r, out_i = _fft_radix2(re, im)
            for k2 in range(FFT_N2):
                u_ref[pl.ds(r, 8), k2 * FFT_N1 + lo:k2 * FFT_N1 + lo + 128] = out_r[k2]
                u_ref[pl.ds(GROUP_DIM + r, 8),
                      k2 * FFT_N1 + lo:k2 * FFT_N1 + lo + 128] = out_i[k2]
        return carry

    lax.fori_loop(0, GROUP_DIM // 8, rows, 0)

    kc = 1024
    for j in range(SEQ // kc):
        lhs_t = u_ref[:, j * kc:(j + 1) * kc].astype(BF16)
        y = lax.dot_general(lhs_t, cs_ref[...], _TN_DIMS, preferred_element_type=F32)
        y_ref[j * kc:(j + 1) * kc, :] = y.astype(BF16)


def _dft_constants():
    n1 = np.arange(FFT_N1, dtype=np.float64)
    k1 = np.arange(FFT_N1, dtype=np.float64)
    n2 = np.arange(FFT_N2, dtype=np.float64)
    n = FFT_N2 * n1[None, :, None] + n2[:, None, None]
    theta = 2.0 * np.pi * np.mod(n * k1[None, None, :], SEQ) / SEQ
    scale_seq = 1.0 / math.sqrt(SEQ)
    g = np.concatenate([np.cos(theta), -np.sin(theta)], axis=-1) * scale_seq
    c = np.arange(GROUP_DIM, dtype=np.float64)
    phi = 2.0 * np.pi * np.mod(c[:, None] * c[None, :], GROUP_DIM) / GROUP_DIM
    scale_ch = 1.0 / math.sqrt(GROUP_DIM)
    cs = np.concatenate([np.cos(phi), np.sin(phi)], axis=0) * scale_ch
    return g.astype(np.float32), cs.astype(np.float32)


def _fft2_real(xin, batch):
    g_np, cs_np = _dft_constants()
    g = jnp.asarray(g_np).astype(BF16)
    cs = jnp.asarray(cs_np).astype(BF16)
    xv = xin.reshape(batch, FFT_N1, FFT_N2 * D_INNER)

    def x_spec(n2):
        return pl.BlockSpec((None, FFT_N1, GROUP_DIM),
                            lambda b, gi: (b, 0, n2 * N_GROUPS + gi))

    return pl.pallas_call(
        _fft2_body,
        grid=(batch, N_GROUPS),
        in_specs=[x_spec(n2) for n2 in range(FFT_N2)] + [
            pl.BlockSpec((FFT_N2, FFT_N1, 2 * FFT_N1), lambda b, gi: (0, 0, 0),
                         pipeline_mode=pl.Buffered(1)),
            pl.BlockSpec((2 * GROUP_DIM, GROUP_DIM), lambda b, gi: (0, 0)),
        ],
        out_specs=pl.BlockSpec((None, SEQ, GROUP_DIM), lambda b, gi: (b, 0, gi)),
        out_shape=jax.ShapeDtypeStruct((batch, SEQ, D_INNER), BF16),
        scratch_shapes=[
            pltpu.VMEM((FFT_N2, GROUP_DIM, 2 * FFT_N1), F32),
            pltpu.VMEM((2 * GROUP_DIM, SEQ), F32),
        ],
        compiler_params=pltpu.CompilerParams(
            dimension_semantics=("arbitrary", "arbitrary"),
            vmem_limit_bytes=56 * 1024 * 1024),
        name="fft2_real",
    )(*([xv] * FFT_N2), g, cs)


def _trunk_body(x_ref, y_ref, sz_ref, wo0_ref, n1_ref, wi1_ref, lng_ref, lnb_ref,
                ws_ref, bs_ref, wo1_ref, nf_ref, o_ref, vn_ref, yb_ref):
    tm = x_ref.shape[0]
    yg = (y_ref[...].astype(F32) * sz_ref[...].astype(F32)).astype(BF16)
    x1 = x_ref[...] + jnp.dot(yg, wo0_ref[...], preferred_element_type=F32)

    h = _rms_norm(x1, n1_ref[...]).astype(BF16)

    v = _gelu_tanh(jnp.dot(h, wi1_ref[:, D_INNER:2 * D_INNER], preferred_element_type=F32))
    mu = jnp.mean(v, axis=-1, keepdims=True)
    vc = v - mu
    var = jnp.mean(vc * vc, axis=-1, keepdims=True)
    vn_ref[...] = (vc * lax.rsqrt(var + EPS) * lng_ref[...] + lnb_ref[...]).astype(BF16)

    for g in range(N_GROUPS):
        cols = slice(g * GROUP_DIM, (g + 1) * GROUP_DIM)
        u = _gelu_tanh(jnp.dot(h, wi1_ref[:, cols], preferred_element_type=F32))
        z = jnp.dot(h, wi1_ref[:, 2 * D_INNER + g * GROUP_DIM:2 * D_INNER + (g + 1) * GROUP_DIM],
                    preferred_element_type=F32)
        uz = u * _silu(z)
        bias = bs_ref[:, g:g + 1]
        for c in range(tm // CHUNK):
            r = slice(c * CHUNK, (c + 1) * CHUNK)
            vm = jnp.dot(ws_ref[g], vn_ref[r, cols], preferred_element_type=F32) + bias
            yb_ref[r, cols] = (uz[r, :] * vm).astype(BF16)

    x2 = x1 + jnp.dot(yb_ref[...], wo1_ref[...], preferred_element_type=F32)
    o_ref[...] = _rms_norm(x2, nf_ref[...])


def _trunk(x2d, y2d, sz, wo0, n1w, wi1, lng, lnb, ws, bs_t, wo1, nfw):
    t = x2d.shape[0]
    tm = TM_TRUNK
    const2 = lambda i: (0, 0)
    single = pl.Buffered(1)
    return pl.pallas_call(
        _trunk_body,
        grid=(t // tm,),
        in_specs=[
            pl.BlockSpec((tm, D_MODEL), lambda i: (i, 0)),
            pl.BlockSpec((tm, D_INNER), lambda i: (i, 0)),
            pl.BlockSpec((tm, D_INNER), lambda i: (i, 0)),
            pl.BlockSpec((D_INNER, D_MODEL), const2, pipeline_mode=single),
            pl.BlockSpec((1, D_MODEL), const2),
            pl.BlockSpec((D_MODEL, 3 * D_INNER), const2, pipeline_mode=single),
            pl.BlockSpec((1, D_INNER), const2),
            pl.BlockSpec((1, D_INNER), const2),
            pl.BlockSpec((N_GROUPS, CHUNK, CHUNK), lambda i: (0, 0, 0)),
            pl.BlockSpec((CHUNK, N_GROUPS), const2),
            pl.BlockSpec((D_INNER, D_MODEL), const2, pipeline_mode=single),
            pl.BlockSpec((1, D_MODEL), const2),
        ],
        out_specs=pl.BlockSpec((tm, D_MODEL), lambda i: (i, 0)),
        out_shape=jax.ShapeDtypeStruct((t, D_MODEL), F32),
        scratch_shapes=[
            pltpu.VMEM((tm, D_INNER), BF16),
            pltpu.VMEM((tm, D_INNER), BF16),
        ],
        compiler_params=pltpu.CompilerParams(
            dimension_semantics=("arbitrary",), vmem_limit_bytes=60 * 1024 * 1024),
        name="trunk",
    )(x2d, y2d, sz, wo0, n1w, wi1, lng, lnb, ws, bs_t, wo1, nfw)


def kernel(x, l0_norm, l0_w_in, l0_w_out, l1_norm, l1_w_in, l1_v_ln_g, l1_v_ln_b,
           l1_w_s, l1_b_s, l1_w_out, final_norm):
    b, s, d = x.shape
    assert (s, d) == (SEQ, D_MODEL)
    x2d = x.reshape(b * s, d)
    row = lambda a: a.reshape(1, -1).astype(F32)

    xin, sz = _in_proj0(x2d, row(l0_norm), l0_w_in.astype(BF16))
    y = _fft2_real(xin, b)
    out = _trunk(
        x2d, y.reshape(b * s, D_INNER), sz,
        l0_w_out.astype(BF16), row(l1_norm), l1_w_in.astype(BF16),
        row(l1_v_ln_g), row(l1_v_ln_b),
        l1_w_s.astype(BF16), l1_b_s.T.astype(F32),
        l1_w_out.astype(BF16), row(final_norm))
    return out.reshape(b, s, d)
```
